```python
import jax, jax.numpy as jnp
from jax import lax
import numpy as np

D_MODEL = 2048
BATCH = 2
SEQ = 8192
DEPTH = 4

D_MIX = D_MODEL
RET_HEADS = 4
RET_HEAD_DIM = 256
D_RET = RET_HEADS * RET_HEAD_DIM
D_SC = D_MIX // 4
D_CF = D_MIX - D_RET - D_SC
SC_WIDTH = 3
CF_WIDTH = 31
RET_CHUNK = 128
ROPE_BASE = 10000.0
EPS = 1e-6
SPLIT_SIZES = [D_RET] * 4 + [D_SC] * 4 + [D_CF] * 3
D_IN = sum(SPLIT_SIZES)
SPLIT_POINTS = [int(s) for s in np.cumsum(SPLIT_SIZES)[:-1]]

kernel_name = "hybrid_retention_shortconv_conformer_adaln"


def rms_norm(x, g):
    xf = x.astype(jnp.float32)
    y = xf * lax.rsqrt(jnp.mean(xf * xf, axis=-1, keepdims=True) + EPS)
    return (y * g.astype(jnp.float32)).astype(x.dtype)


def layer_norm(x, g, b):
    xf = x.astype(jnp.float32)
    mu = jnp.mean(xf, axis=-1, keepdims=True)
    xc = xf - mu
    y = xc * lax.rsqrt(jnp.mean(xc * xc, axis=-1, keepdims=True) + EPS)
    return (y * g.astype(jnp.float32) + b.astype(jnp.float32)).astype(x.dtype)


def causal_dwconv(u, w):
    K, C = w.shape
    return lax.conv_general_dilated(
        u, w[:, None, :].astype(u.dtype), window_strides=(1,), padding=[(K - 1, 0)],
        dimension_numbers=("NWC", "WIO", "NWC"), feature_group_count=C)


def rotary(t, positions):
    d = t.shape[-1]
    inv_freq = ROPE_BASE ** (-jnp.arange(0, d // 2, dtype=jnp.float32) / (d // 2))
    ang = positions.astype(jnp.float32)[..., None] * inv_freq
    cos = jnp.cos(ang)[:, :, None, :]
    sin = jnp.sin(ang)[:, :, None, :]
    t1, t2 = t[..., : d // 2], t[..., d // 2:]
    return jnp.concatenate([t1 * cos - t2 * sin, t2 * cos + t1 * sin], axis=-1)


def retention_chunkwise(q, k, v):
    Bsz, T, H, dk = q.shape
    dv = v.shape[-1]
    n_chunks = T // RET_CHUNK
    log_gamma = jnp.log1p(-jnp.exp2(-5.0 - jnp.arange(H, dtype=jnp.float32)))
    idx = jnp.arange(RET_CHUNK, dtype=jnp.float32)
    diff = idx[:, None] - idx[None, :]
    intra = jnp.where(diff[None] >= 0,
                      jnp.exp(jnp.maximum(diff, 0.0)[None] * log_gamma[:, None, None]), 0.0)
    q_decay = jnp.exp((idx[None] + 1.0) * log_gamma[:, None])[None, :, :, None]
    k_decay = jnp.exp((RET_CHUNK - 1.0 - idx[None]) * log_gamma[:, None])[None, :, :, None]
    chunk_decay = jnp.exp(RET_CHUNK * log_gamma)[None, :, None, None]

    def to_chunks(t):
        return t.reshape(Bsz, n_chunks, RET_CHUNK, H, t.shape[-1]).transpose(1, 0, 3, 2, 4)

    def step(state, qkv):
        qc, kc, vc = qkv
        scores = jnp.einsum("bhid,bhjd->bhij", qc, kc) * intra[None]
        o_inner = jnp.einsum("bhij,bhjv->bhiv", scores, vc)
        o_cross = jnp.einsum("bhid,bhdv->bhiv", qc, state) * q_decay
        state = state * chunk_decay + jnp.einsum("bhjd,bhjv->bhdv", kc * k_decay, vc)
        return state, o_inner + o_cross

    state0 = jnp.zeros((Bsz, H, dk, dv), jnp.float32)
    _, out = lax.scan(step, state0, (to_chunks(q), to_chunks(k), to_chunks(v)))
    return out.transpose(1, 0, 3, 2, 4).reshape(Bsz, T, H, dv)


def group_norm_heads(o):
    mu = jnp.mean(o, axis=-1, keepdims=True)
    oc = o - mu
    return oc * lax.rsqrt(jnp.mean(oc * oc, axis=-1, keepdims=True) + EPS)


def hybrid_layer(x, c, positions, ada_w, ada_b, norm_g, w_in, sc_conv_w,
                 cf_conv_w, cf_conv_b, cf_ln_g, cf_ln_b, w_out):
    Bsz, T, _ = x.shape
    mod = jax.nn.silu(c) @ ada_w + ada_b
    shift, scale, gate = jnp.split(mod, 3, axis=-1)
    h = rms_norm(x, norm_g) * (1.0 + scale[:, None, :]) + shift[:, None, :]

    proj = h @ w_in
    (r_q, r_k, r_v, r_g, s_b, s_c, s_h, s_g, f_a, f_b, f_g) = jnp.split(proj, SPLIT_POINTS, axis=-1)

    hs = (Bsz, T, RET_HEADS, RET_HEAD_DIM)
    q = rotary(r_q.reshape(hs).astype(jnp.float32), positions)
    k = rotary(r_k.reshape(hs).astype(jnp.float32), positions) * (RET_HEAD_DIM ** -0.5)
    v = r_v.reshape(hs).astype(jnp.float32)
    ret = group_norm_heads(retention_chunkwise(q, k, v)).reshape(Bsz, T, D_RET).astype(x.dtype)
    ret = jax.nn.silu(r_g) * ret

    sc = s_b * causal_dwconv(s_c * s_h, sc_conv_w)
    sc = jax.nn.silu(s_g) * sc

    glu = f_a * jax.nn.sigmoid(f_b)
    cf = causal_dwconv(glu, cf_conv_w) + cf_conv_b
    cf = jax.nn.silu(layer_norm(cf, cf_ln_g, cf_ln_b))
    cf = jax.nn.silu(f_g) * cf

    mixed = jnp.concatenate([ret, sc, cf], axis=-1) @ w_out
    return x + gate[:, None, :] * mixed


def setup_inputs(seed: int = 0) -> dict:
    key = jax.random.key(seed)
    ks = jax.random.split(key, 14)
    f32 = jnp.float32
    x = jax.random.normal(ks[0], (BATCH, SEQ, D_MODEL), f32)
    c = jax.random.normal(ks[1], (BATCH, D_MODEL), f32)
    positions = jnp.broadcast_to(jnp.arange(SEQ, dtype=jnp.int32)[None, :], (BATCH, SEQ))
    ada_w = jax.random.normal(ks[2], (DEPTH, D_MODEL, 3 * D_MODEL), f32) * (0.5 * D_MODEL ** -0.5)
    ada_b = jax.random.normal(ks[3], (DEPTH, 3 * D_MODEL), f32) * 0.02
    norm_g = 1.0 + 0.02 * jax.random.normal(ks[4], (DEPTH, D_MODEL), f32)
    w_in = jax.random.normal(ks[5], (DEPTH, D_MODEL, D_IN), f32) * (D_MODEL ** -0.5)
    sc_conv_w = jax.random.normal(ks[6], (DEPTH, SC_WIDTH, D_SC), f32) * (SC_WIDTH ** -0.5)
    cf_conv_w = jax.random.normal(ks[7], (DEPTH, CF_WIDTH, D_CF), f32) * (CF_WIDTH ** -0.5)
    cf_conv_b = jax.random.normal(ks[8], (DEPTH, D_CF), f32) * 0.02
    cf_ln_g = 1.0 + 0.02 * jax.random.normal(ks[9], (DEPTH, D_CF), f32)
    cf_ln_b = jax.random.normal(ks[10], (DEPTH, D_CF), f32) * 0.02
    w_out = jax.random.normal(ks[11], (DEPTH, D_MIX, D_MODEL), f32) * (D_MIX ** -0.5)
    final_g = 1.0 + 0.02 * jax.random.normal(ks[12], (D_MODEL,), f32)
    return {"x": x, "c": c, "positions": positions, "ada_w": ada_w, "ada_b": ada_b,
            "norm_g": norm_g, "w_in": w_in, "sc_conv_w": sc_conv_w, "cf_conv_w": cf_conv_w,
            "cf_conv_b": cf_conv_b, "cf_ln_g": cf_ln_g, "cf_ln_b": cf_ln_b, "w_out": w_out,
            "final_g": final_g}


def reference(x, c, positions, ada_w, ada_b, norm_g, w_in, sc_conv_w, cf_conv_w,
              cf_conv_b, cf_ln_g, cf_ln_b, w_out, final_g):
    h = x
    for layer in range(DEPTH):
        h = hybrid_layer(h, c, positions, ada_w[layer], ada_b[layer], norm_g[layer], w_in[layer],
                         sc_conv_w[layer], cf_conv_w[layer], cf_conv_b[layer], cf_ln_g[layer],
                         cf_ln_b[layer], w_out[layer])
    return rms_norm(h, final_g)
```

```python
import functools

import jax
import jax.numpy as jnp
import numpy as np
from jax import lax
from jax.experimental import pallas as pl
from jax.experimental.pallas import tpu as pltpu

D_MODEL = 2048
RET_HEADS = 4
RET_HEAD_DIM = 256
HALF_DIM = RET_HEAD_DIM // 2
D_RET = RET_HEADS * RET_HEAD_DIM
D_SC = 512
D_CF = 512
SC_WIDTH = 3
CF_WIDTH = 31
D_IN = 4 * D_RET + 4 * D_SC + 3 * D_CF
ROPE_BASE = 10000.0
EPS = 1e-6

OFF_Q, OFF_K, OFF_V, OFF_G = 0, D_RET, 2 * D_RET, 3 * D_RET
OFF_SB = 4 * D_RET
OFF_SC, OFF_SH, OFF_SG = OFF_SB + D_SC, OFF_SB + 2 * D_SC, OFF_SB + 3 * D_SC
OFF_FA = OFF_SB + 4 * D_SC
OFF_FB, OFF_FG = OFF_FA + D_CF, OFF_FA + 2 * D_CF

V7X_SUBLANES = 8
V7X_LANES = 128
V7X_VMEM_LIMIT_BYTES = 56 * 1024 * 1024

RET_CHUNK = 256
TILE_T = RET_CHUNK
TILE_M = 1024
TILE_N = 1280
ADA_TILE_N = 1024
ROPE_TILE_T = 1024
NORM_ROWS = 16
CONV_ROWS = 32
SC_HALO = V7X_SUBLANES
CF_HALO = 32

F32 = jnp.float32
BF16 = jnp.bfloat16


def _sigmoid(v):
    return 1.0 / (1.0 + jnp.exp(-v))


def _silu(v):
    return v * _sigmoid(v)


def _ada_kernel(c_ref, w_ref, b_ref, o_ref):
    c = c_ref[...]
    sc = _silu(c).astype(BF16)
    o_ref[0] = jnp.dot(sc, w_ref[0].astype(BF16), preferred_element_type=F32) + b_ref[0]


def _ada_mod(c_pad, ada_w, ada_b):
    depth, d, n = ada_w.shape
    rows = c_pad.shape[0]
    return pl.pallas_call(
        _ada_kernel,
        grid=(depth, n // ADA_TILE_N),
        in_specs=[
            pl.BlockSpec((rows, d), lambda l, j: (0, 0)),
            pl.BlockSpec((1, d, ADA_TILE_N), lambda l, j: (l, 0, j)),
            pl.BlockSpec((1, 1, ADA_TILE_N), lambda l, j: (l, 0, j)),
        ],
        out_specs=pl.BlockSpec((1, rows, ADA_TILE_N), lambda l, j: (l, 0, j)),
        out_shape=jax.ShapeDtypeStruct((depth, rows, n), F32),
        compiler_params=pltpu.CompilerParams(
            dimension_semantics=("arbitrary", "arbitrary"),
            vmem_limit_bytes=V7X_VMEM_LIMIT_BYTES),
        name="ada_mod",
    )(c_pad, ada_w, ada_b.reshape(depth, 1, n))


def _rope_kernel(pos_ref, invf_ref, cos_ref, sin_ref):
    ang = pos_ref[0].astype(F32) * invf_ref[...]
    cos_ref[0] = jnp.cos(ang)
    sin_ref[0] = jnp.sin(ang)


def _rope_tables(positions, inv_freq):
    bsz, t = positions.shape
    out = jax.ShapeDtypeStruct((bsz, t, HALF_DIM), F32)
    return pl.pallas_call(
        _rope_kernel,
        grid=(bsz, t // ROPE_TILE_T),
        in_specs=[
            pl.BlockSpec((1, ROPE_TILE_T, 1), lambda b, i: (b, i, 0)),
            pl.BlockSpec((1, HALF_DIM), lambda b, i: (0, 0)),
        ],
        out_specs=[pl.BlockSpec((1, ROPE_TILE_T, HALF_DIM), lambda b, i: (b, i, 0))] * 2,
        out_shape=[out, out],
        compiler_params=pltpu.CompilerParams(
            dimension_semantics=("arbitrary", "arbitrary"),
            vmem_limit_bytes=V7X_VMEM_LIMIT_BYTES),
        name="rope_tables",
    )(positions.reshape(bsz, t, 1), inv_freq.reshape(1, HALF_DIM))


def _inproj_kernel(x_ref, scale_ref, shift_ref, g_ref, w_ref, o_ref, h_ref):
    @pl.when(pl.program_id(2) == 0)
    def _():
        g = g_ref[...]
        mul = 1.0 + scale_ref[0]
        add = shift_ref[0]

        def body(i, carry):
            r = pl.multiple_of(i * NORM_ROWS, NORM_ROWS)
            xv = x_ref[0, pl.ds(r, NORM_ROWS), :]
            ms = jnp.mean(xv * xv, axis=-1, keepdims=True)
            y = xv * lax.rsqrt(ms + EPS) * g
            h_ref[pl.ds(r, NORM_ROWS), :] = (y * mul + add).astype(BF16)
            return carry

        lax.fori_loop(0, TILE_M // NORM_ROWS, body, 0)

    o_ref[0] = jnp.dot(h_ref[...], w_ref[...], preferred_element_type=F32).astype(BF16)


def _in_projection(x, scale, shift, norm_g, w_in_bf16):
    bsz, t, d = x.shape
    n = w_in_bf16.shape[1]
    return pl.pallas_call(
        _inproj_kernel,
        grid=(bsz, t // TILE_M, n // TILE_N),
        in_specs=[
            pl.BlockSpec((1, TILE_M, d), lambda b, i, j: (b, i, 0)),
            pl.BlockSpec((1, 1, d), lambda b, i, j: (b, 0, 0)),
            pl.BlockSpec((1, 1, d), lambda b, i, j: (b, 0, 0)),
            pl.BlockSpec((1, d), lambda b, i, j: (0, 0)),
            pl.BlockSpec((d, TILE_N), lambda b, i, j: (0, j)),
        ],
        out_specs=pl.BlockSpec((1, TILE_M, TILE_N), lambda b, i, j: (b, i, j)),
        out_shape=jax.ShapeDtypeStruct((bsz, t, n), BF16),
        scratch_shapes=[pltpu.VMEM((TILE_M, d), BF16)],
        compiler_params=pltpu.CompilerParams(
            dimension_semantics=("arbitrary", "arbitrary", "arbitrary"),
            vmem_limit_bytes=V7X_VMEM_LIMIT_BYTES),
        name="in_projection",
    )(x, scale, shift, norm_g.reshape(1, d), w_in_bf16)


def _mixer_kernel(cdec_ref, x_ref, p_ref, cos_ref, sin_ref, gate_ref, wout_ref,
                  scw_ref, cfw_ref, cfb_ref, lng_ref, lnb_ref,
                  intra_ref, qdec_ref, kdec_ref, fing_ref,
                  o_ref,
                  state_ref, mixed_ref, ubuf_ref, gbuf_ref, *, final_norm):
    t_idx = pl.program_id(1)

    @pl.when(t_idx == 0)
    def _():
        state_ref[...] = jnp.zeros_like(state_ref)
        ubuf_ref[0:SC_HALO, :] = jnp.zeros((SC_HALO, D_SC), F32)
        gbuf_ref[0:CF_HALO, :] = jnp.zeros((CF_HALO, D_CF), F32)

    cos = cos_ref[0]
    sin = sin_ref[0]

    def rope(off):
        t1 = p_ref[0, :, off:off + HALF_DIM].astype(F32)
        t2 = p_ref[0, :, off + HALF_DIM:off + RET_HEAD_DIM].astype(F32)
        return t1 * cos - t2 * sin, t2 * cos + t1 * sin

    for h in range(RET_HEADS):
        lo = h * RET_HEAD_DIM
        q1, q2 = rope(OFF_Q + lo)
        k1, k2 = rope(OFF_K + lo)
        k_scale = RET_HEAD_DIM ** -0.5
        k1 = k1 * k_scale
        k2 = k2 * k_scale
        q = jnp.concatenate([q1, q2], axis=-1).astype(BF16)
        k = jnp.concatenate([k1, k2], axis=-1).astype(BF16)
        kd_tab = kdec_ref[h]
        kd = jnp.concatenate([k1 * kd_tab, k2 * kd_tab], axis=-1).astype(BF16)
        v = p_ref[0, :, OFF_V + lo:OFF_V + lo + RET_HEAD_DIM]

        scores = lax.dot_general(q, k, (((1,), (1,)), ((), ())), preferred_element_type=F32)
        scores = (scores * intra_ref[h]).astype(BF16)
        o_inner = jnp.dot(scores, v, preferred_element_type=F32)
        st = state_ref[h]
        o_cross = jnp.dot(q, st.astype(BF16), preferred_element_type=F32)
        qd_tab = qdec_ref[h]
        o = o_inner + jnp.concatenate(
            [o_cross[:, :HALF_DIM] * qd_tab, o_cross[:, HALF_DIM:] * qd_tab], axis=-1)
        state_ref[h] = st * cdec_ref[h] + lax.dot_general(
            kd, v, (((0,), (0,)), ((), ())), preferred_element_type=F32)

        mu = jnp.mean(o, axis=-1, keepdims=True)
        oc = o - mu
        var = jnp.mean(oc * oc, axis=-1, keepdims=True)
        on = oc * lax.rsqrt(var + EPS)
        g = p_ref[0, :, OFF_G + lo:OFF_G + lo + RET_HEAD_DIM].astype(F32)
        mixed_ref[:, lo:lo + RET_HEAD_DIM] = (_silu(g) * on).astype(BF16)

    u = (p_ref[0, :, OFF_SC:OFF_SC + D_SC].astype(F32)
         * p_ref[0, :, OFF_SH:OFF_SH + D_SC].astype(F32))
    ubuf_ref[SC_HALO:SC_HALO + TILE_T, :] = u
    conv = scw_ref[SC_WIDTH - 1:SC_WIDTH, :] * u
    for kk in range(SC_WIDTH - 1):
        shift = SC_WIDTH - 1 - kk
        conv = conv + scw_ref[kk:kk + 1, :] * ubuf_ref[SC_HALO - shift:SC_HALO - shift + TILE_T, :]
    ubuf_ref[0:SC_HALO, :] = ubuf_ref[TILE_T:TILE_T + SC_HALO, :]
    s_b = p_ref[0, :, OFF_SB:OFF_SB + D_SC].astype(F32)
    s_g = p_ref[0, :, OFF_SG:OFF_SG + D_SC].astype(F32)
    mixed_ref[:, D_RET:D_RET + D_SC] = (_silu(s_g) * (s_b * conv)).astype(BF16)

    f_a = p_ref[0, :, OFF_FA:OFF_FA + D_CF].astype(F32)
    f_b = p_ref[0, :, OFF_FB:OFF_FB + D_CF].astype(F32)
    gbuf_ref[CF_HALO:CF_HALO + TILE_T, :] = f_a * _sigmoid(f_b)
    bias = cfb_ref[...]
    ln_g = lng_ref[...]
    ln_b = lnb_ref[...]
    base = CF_HALO - (CF_WIDTH - 1)
    for rc in range(TILE_T // CONV_ROWS):
        r0 = rc * CONV_ROWS
        acc = jnp.zeros((CONV_ROWS, D_CF), F32) + bias
        for kk in range(CF_WIDTH):
            acc = acc + cfw_ref[kk:kk + 1, :] * gbuf_ref[r0 + base + kk:r0 + base + kk + CONV_ROWS, :]
        mu = jnp.mean(acc, axis=-1, keepdims=True)
        ac = acc - mu
        var = jnp.mean(ac * ac, axis=-1, keepdims=True)
        y = ac * lax.rsqrt(var + EPS) * ln_g + ln_b
        f_g = p_ref[0, r0:r0 + CONV_ROWS, OFF_FG:OFF_FG + D_CF].astype(F32)
        mixed_ref[r0:r0 + CONV_ROWS, D_RET + D_SC:D_RET + D_SC + D_CF] = (
            _silu(f_g) * _silu(y)).astype(BF16)
    gbuf_ref[0:CF_HALO, :] = gbuf_ref[TILE_T:TILE_T + CF_HALO, :]

    y = jnp.dot(mixed_ref[...], wout_ref[...], preferred_element_type=F32)
    res = x_ref[0] + gate_ref[0] * y
    if final_norm:
        ms = jnp.mean(res * res, axis=-1, keepdims=True)
        res = res * lax.rsqrt(ms + EPS) * fing_ref[...]
    o_ref[0] = res


def _mixer_layer(x, proj, cos, sin, gate, w_out_bf16, sc_w, cf_w, cf_b, ln_g, ln_b,
                 intra, qdec, kdec, cdec, final_g, final_norm):
    bsz, t, d = x.shape
    const2 = lambda b, i: (0, 0)
    const3 = lambda b, i: (0, 0, 0)
    row = lambda b, i: (b, i, 0)
    return pl.pallas_call(
        functools.partial(_mixer_kernel, final_norm=final_norm),
        grid=(bsz, t // TILE_T),
        in_specs=[
            pl.BlockSpec(memory_space=pltpu.SMEM),
            pl.BlockSpec((1, TILE_T, d), row),
            pl.BlockSpec((1, TILE_T, D_IN), row),
            pl.BlockSpec((1, TILE_T, HALF_DIM), row),
            pl.BlockSpec((1, TILE_T, HALF_DIM), row),
            pl.BlockSpec((1, 1, d), lambda b, i: (b, 0, 0)),
            pl.BlockSpec((d, d), const2),
            pl.BlockSpec((SC_WIDTH, D_SC), const2),
            pl.BlockSpec((CF_WIDTH, D_CF), const2),
            pl.BlockSpec((1, D_CF), const2),
            pl.BlockSpec((1, D_CF), const2),
            pl.BlockSpec((1, D_CF), const2),
            pl.BlockSpec((RET_HEADS, RET_CHUNK, RET_CHUNK), const3),
            pl.BlockSpec((RET_HEADS, RET_CHUNK, HALF_DIM), const3),
            pl.BlockSpec((RET_HEADS, RET_CHUNK, HALF_DIM), const3),
            pl.BlockSpec((1, d), const2),
        ],
        out_specs=pl.BlockSpec((1, TILE_T, d), row),
        out_shape=jax.ShapeDtypeStruct((bsz, t, d), F32),
        scratch_shapes=[
            pltpu.VMEM((RET_HEADS, RET_HEAD_DIM, RET_HEAD_DIM), F32),
            pltpu.VMEM((TILE_T, d), BF16),
            pltpu.VMEM((SC_HALO + TILE_T, D_SC), F32),
            pltpu.VMEM((CF_HALO + TILE_T, D_CF), F32),
        ],
        compiler_params=pltpu.CompilerParams(
            dimension_semantics=("arbitrary", "arbitrary"),
            vmem_limit_bytes=V7X_VMEM_LIMIT_BYTES),
        name="mixer_layer",
    )(cdec, x, proj, cos, sin, gate, w_out_bf16, sc_w, cf_w, cf_b.reshape(1, D_CF),
      ln_g.reshape(1, D_CF), ln_b.reshape(1, D_CF), intra, qdec, kdec, final_g.reshape(1, d))


def _decay_tables():
    log_gamma = jnp.log1p(-jnp.exp2(-5.0 - jnp.arange(RET_HEADS, dtype=F32)))
    idx = jnp.arange(RET_CHUNK, dtype=F32)
    diff = idx[:, None] - idx[None, :]
    intra = jnp.where(diff[None] >= 0,
                      jnp.exp(jnp.maximum(diff, 0.0)[None] * log_gamma[:, None, None]), 0.0)
    qdec = jnp.exp((idx[None] + 1.0) * log_gamma[:, None])
    kdec = jnp.exp((RET_CHUNK - 1.0 - idx[None]) * log_gamma[:, None])
    cdec = jnp.exp(RET_CHUNK * log_gamma)
    bcast = lambda a: jnp.broadcast_to(a[:, :, None], (RET_HEADS, RET_CHUNK, HALF_DIM))
    return intra, bcast(qdec), bcast(kdec), cdec


def kernel(x, c, positions, ada_w, ada_b, norm_g, w_in, sc_conv_w, cf_conv_w, cf_conv_b, cf_ln_g, cf_ln_b, w_out, final_g):
    bsz, t, d = x.shape
    depth = ada_w.shape[0]
    assert d == D_MODEL and w_in.shape[2] == D_IN
    assert t % TILE_M == 0 and t % TILE_T == 0 and t % ROPE_TILE_T == 0 and D_IN % TILE_N == 0

    c_pad = jnp.zeros((V7X_SUBLANES, d), F32).at[:bsz].set(c)
    mod = _ada_mod(c_pad, ada_w, ada_b)[:, :bsz]
    shift = mod[:, :, :d].reshape(depth, bsz, 1, d)
    scale = mod[:, :, d:2 * d].reshape(depth, bsz, 1, d)
    gate = mod[:, :, 2 * d:].reshape(depth, bsz, 1, d)

    inv_freq = ROPE_BASE ** (-jnp.arange(0, HALF_DIM, dtype=F32) / HALF_DIM)
    cos, sin = _rope_tables(positions, inv_freq)
    intra, qdec, kdec, cdec = _decay_tables()

    w_in_bf16 = w_in.astype(BF16)
    w_out_bf16 = w_out.astype(BF16)

    h = x
    for layer in range(depth):
        proj = _in_projection(h, scale[layer], shift[layer], norm_g[layer], w_in_bf16[layer])
        h = _mixer_layer(h, proj, cos, sin, gate[layer], w_out_bf16[layer], sc_conv_w[layer],
                         cf_conv_w[layer], cf_conv_b[layer], cf_ln_g[layer], cf_ln_b[layer],
                         intra, qdec, kdec, cdec, final_g, final_norm=(layer == depth - 1))
    return h
```

```python
import functools

import jax
import jax.numpy as jnp
import numpy as np
from jax import lax
from jax.experimental import pallas as pl
from jax.experimental.pallas import tpu as pltpu

D_MODEL = 2048
RET_HEADS = 4
RET_HEAD_DIM = 256
HALF_DIM = RET_HEAD_DIM // 2
D_RET = RET_HEADS * RET_HEAD_DIM
D_SC = 512
D_CF = 512
SC_WIDTH = 3
CF_WIDTH = 31
D_IN = 4 * D_RET + 4 * D_SC + 3 * D_CF
ROPE_BASE = 10000.0
EPS = 1e-6

OFF_Q, OFF_K, OFF_V, OFF_G = 0, D_RET, 2 * D_RET, 3 * D_RET
OFF_SB = 4 * D_RET
OFF_SC, OFF_SH, OFF_SG = OFF_SB + D_SC, OFF_SB + 2 * D_SC, OFF_SB + 3 * D_SC
OFF_FA = OFF_SB + 4 * D_SC
OFF_FB, OFF_FG = OFF_FA + D_CF, OFF_FA + 2 * D_CF

V7X_SUBLANES = 8
V7X_LANES = 128
V7X_VMEM_LIMIT_BYTES = 56 * 1024 * 1024

RET_CHUNK = 256
TILE_T = RET_CHUNK
TILE_M = 1024
TILE_N = 1280
ADA_TILE_N = 1024
ROPE_TILE_T = 1024
NORM_ROWS = 16
NORM_CHUNK = 256
NORM_CHUNKS = TILE_M // NORM_CHUNK
CONV_ROWS = 32
SC_HALO = V7X_SUBLANES
CF_HALO = 32
CF_ROWS = CF_HALO + TILE_T
assert D_IN // TILE_N >= NORM_CHUNKS and CF_HALO % V7X_SUBLANES == 0 and CF_HALO >= CF_WIDTH

F32 = jnp.float32
BF16 = jnp.bfloat16


def _sigmoid(v):
    return 1.0 / (1.0 + jnp.exp(-v))


def _silu(v):
    return v * _sigmoid(v)


def _ada_kernel(c_ref, w_ref, b_ref, o_ref):
    c = c_ref[...]
    sc = _silu(c).astype(BF16)
    o_ref[0] = jnp.dot(sc, w_ref[0].astype(BF16), preferred_element_type=F32) + b_ref[0]


def _ada_mod(c_pad, ada_w, ada_b):
    depth, d, n = ada_w.shape
    rows = c_pad.shape[0]
    return pl.pallas_call(
        _ada_kernel,
        grid=(depth, n // ADA_TILE_N),
        in_specs=[
            pl.BlockSpec((rows, d), lambda l, j: (0, 0)),
            pl.BlockSpec((1, d, ADA_TILE_N), lambda l, j: (l, 0, j)),
            pl.BlockSpec((1, 1, ADA_TILE_N), lambda l, j: (l, 0, j)),
        ],
        out_specs=pl.BlockSpec((1, rows, ADA_TILE_N), lambda l, j: (l, 0, j)),
        out_shape=jax.ShapeDtypeStruct((depth, rows, n), F32),
        compiler_params=pltpu.CompilerParams(
            dimension_semantics=("arbitrary", "arbitrary"),
            vmem_limit_bytes=V7X_VMEM_LIMIT_BYTES),
        name="ada_mod",
    )(c_pad, ada_w, ada_b.reshape(depth, 1, n))


def _rope_kernel(pos_ref, invf_ref, cos_ref, sin_ref):
    ang = pos_ref[0].astype(F32) * invf_ref[...]
    cos_ref[0] = jnp.cos(ang)
    sin_ref[0] = jnp.sin(ang)


def _rope_tables(positions, inv_freq):
    bsz, t = positions.shape
    out = jax.ShapeDtypeStruct((bsz, t, HALF_DIM), F32)
    return pl.pallas_call(
        _rope_kernel,
        grid=(bsz, t // ROPE_TILE_T),
        in_specs=[
            pl.BlockSpec((1, ROPE_TILE_T, 1), lambda b, i: (b, i, 0)),
            pl.BlockSpec((1, HALF_DIM), lambda b, i: (0, 0)),
        ],
        out_specs=[pl.BlockSpec((1, ROPE_TILE_T, HALF_DIM), lambda b, i: (b, i, 0))] * 2,
        out_shape=[out, out],
        compiler_params=pltpu.CompilerParams(
            dimension_semantics=("arbitrary", "arbitrary"),
            vmem_limit_bytes=V7X_VMEM_LIMIT_BYTES),
        name="rope_tables",
    )(positions.reshape(bsz, t, 1), inv_freq.reshape(1, HALF_DIM))


def _inproj_kernel(x_ref, scale_ref, shift_ref, g_ref, w_ref, o_ref, h_even_ref, h_odd_ref):
    s = pl.program_id(0)
    chunk = jnp.minimum(pl.program_id(1), NORM_CHUNKS - 1)
    row0 = pl.multiple_of(chunk * NORM_CHUNK, NORM_CHUNK)

    def normalise(h_write):
        g = g_ref[...]
        mul = 1.0 + scale_ref[0]
        add = shift_ref[0]
        for r in range(0, NORM_CHUNK, NORM_ROWS):
            xv = x_ref[r:r + NORM_ROWS, :]
            ms = jnp.mean(xv * xv, axis=-1, keepdims=True)
            y = xv * lax.rsqrt(ms + EPS) * g
            h_write[pl.ds(row0 + r, NORM_ROWS), :] = (y * mul + add).astype(BF16)

    def project(h_read):
        o_ref[...] = jnp.dot(h_read[...], w_ref[0], preferred_element_type=F32).astype(BF16)

    @pl.when(s == 0)
    def _():
        normalise(h_even_ref)

    @pl.when(s % 2 == 1)
    def _():
        project(h_even_ref)
        normalise(h_odd_ref)

    @pl.when((s % 2 == 0) & (s > 0))
    def _():
        project(h_odd_ref)
        normalise(h_even_ref)


def _in_projection(x, scale, shift, norm_g, w_in_bf16, layer):
    bsz, t, d = x.shape
    n = w_in_bf16.shape[2]
    m = bsz * t
    n_tiles = m // TILE_M
    tiles_per_seq = t // TILE_M

    def x_map(s, j):
        return (jnp.minimum(s, n_tiles - 1) * NORM_CHUNKS + jnp.minimum(j, NORM_CHUNKS - 1), 0)

    def mod_map(s, j):
        return (jnp.minimum(s, n_tiles - 1) // tiles_per_seq, 0, 0)

    def out_map(s, j):
        return (jnp.maximum(s - 1, 0), jnp.where(s == 0, 0, j))

    proj = pl.pallas_call(
        _inproj_kernel,
        grid=(n_tiles + 1, n // TILE_N),
        in_specs=[
            pl.BlockSpec((NORM_CHUNK, d), x_map),
            pl.BlockSpec((1, 1, d), mod_map),
            pl.BlockSpec((1, 1, d), mod_map),
            pl.BlockSpec((1, d), lambda s, j: (0, 0)),
            pl.BlockSpec((1, d, TILE_N), lambda s, j: (layer, 0, j)),
        ],
        out_specs=pl.BlockSpec((TILE_M, TILE_N), out_map),
        out_shape=jax.ShapeDtypeStruct((m, n), BF16),
        scratch_shapes=[pltpu.VMEM((TILE_M, d), BF16), pltpu.VMEM((TILE_M, d), BF16)],
        compiler_params=pltpu.CompilerParams(
            dimension_semantics=("arbitrary", "arbitrary"),
            vmem_limit_bytes=V7X_VMEM_LIMIT_BYTES),
        name="in_projection",
    )(x.reshape(m, d), scale, shift, norm_g.reshape(1, d), w_in_bf16)
    return proj.reshape(bsz, t, n)


def _mixer_kernel(cdec_ref, x_ref, p_ref, cos_ref, sin_ref, gate_ref, wout_ref,
                  scw_ref, cfw_ref, cfb_ref, lng_ref, lnb_ref,
                  intra_ref, qdec_ref, kdec_ref, fing_ref,
                  o_ref,
                  state_ref, mixed_ref, ubuf_ref, gbuf_ref, gsh_ref, *, final_norm):
    t_idx = pl.program_id(1)

    @pl.when(t_idx == 0)
    def _():
        state_ref[...] = jnp.zeros_like(state_ref)
        ubuf_ref[0:SC_HALO, :] = jnp.zeros((SC_HALO, D_SC), F32)
        gbuf_ref[0:CF_HALO, :] = jnp.zeros((CF_HALO, D_CF), F32)

    cos = cos_ref[0]
    sin = sin_ref[0]

    def rope(off):
        t1 = p_ref[0, :, off:off + HALF_DIM].astype(F32)
        t2 = p_ref[0, :, off + HALF_DIM:off + RET_HEAD_DIM].astype(F32)
        return t1 * cos - t2 * sin, t2 * cos + t1 * sin

    for h in range(RET_HEADS):
        lo = h * RET_HEAD_DIM
        q1, q2 = rope(OFF_Q + lo)
        k1, k2 = rope(OFF_K + lo)
        k_scale = RET_HEAD_DIM ** -0.5
        k1 = k1 * k_scale
        k2 = k2 * k_scale
        q = jnp.concatenate([q1, q2], axis=-1).astype(BF16)
        k = jnp.concatenate([k1, k2], axis=-1).astype(BF16)
        kd_tab = kdec_ref[h]
        kd = jnp.concatenate([k1 * kd_tab, k2 * kd_tab], axis=-1).astype(BF16)
        v = p_ref[0, :, OFF_V + lo:OFF_V + lo + RET_HEAD_DIM]

        scores = lax.dot_general(q, k, (((1,), (1,)), ((), ())), preferred_element_type=F32)
        scores = (scores * intra_ref[h]).astype(BF16)
        o_inner = jnp.dot(scores, v, preferred_element_type=F32)
        st = state_ref[h]
        o_cross = jnp.dot(q, st.astype(BF16), preferred_element_type=F32)
        qd_tab = qdec_ref[h]
        o = o_inner + jnp.concatenate(
            [o_cross[:, :HALF_DIM] * qd_tab, o_cross[:, HALF_DIM:] * qd_tab], axis=-1)
        state_ref[h] = st * cdec_ref[h] + lax.dot_general(
            kd, v, (((0,), (0,)), ((), ())), preferred_element_type=F32)

        mu = jnp.mean(o, axis=-1, keepdims=True)
        oc = o - mu
        var = jnp.mean(oc * oc, axis=-1, keepdims=True)
        on = oc * lax.rsqrt(var + EPS)
        g = p_ref[0, :, OFF_G + lo:OFF_G + lo + RET_HEAD_DIM].astype(F32)
        mixed_ref[:, lo:lo + RET_HEAD_DIM] = (_silu(g) * on).astype(BF16)

    u = (p_ref[0, :, OFF_SC:OFF_SC + D_SC].astype(F32)
         * p_ref[0, :, OFF_SH:OFF_SH + D_SC].astype(F32))
    ubuf_ref[SC_HALO:SC_HALO + TILE_T, :] = u
    u_all = ubuf_ref[...]
    conv = scw_ref[SC_WIDTH - 1:SC_WIDTH, :] * u
    for kk in range(SC_WIDTH - 1):
        shift = SC_WIDTH - 1 - kk
        conv = conv + scw_ref[kk:kk + 1, :] * pltpu.roll(u_all, shift, axis=0)[SC_HALO:, :]
    ubuf_ref[0:SC_HALO, :] = ubuf_ref[TILE_T:TILE_T + SC_HALO, :]
    s_b = p_ref[0, :, OFF_SB:OFF_SB + D_SC].astype(F32)
    s_g = p_ref[0, :, OFF_SG:OFF_SG + D_SC].astype(F32)
    mixed_ref[:, D_RET:D_RET + D_SC] = (_silu(s_g) * (s_b * conv)).astype(BF16)

    f_a = p_ref[0, :, OFF_FA:OFF_FA + D_CF].astype(F32)
    f_b = p_ref[0, :, OFF_FB:OFF_FB + D_CF].astype(F32)
    gbuf_ref[CF_HALO:CF_ROWS, :] = f_a * _sigmoid(f_b)
    g_all = gbuf_ref[...]
    for b in range(1, V7X_SUBLANES):
        gsh_ref[b - 1] = pltpu.roll(g_all, CF_ROWS - b, axis=0)
    bias = cfb_ref[...]
    ln_g = lng_ref[...]
    ln_b = lnb_ref[...]
    base = CF_HALO - (CF_WIDTH - 1)
    for rc in range(TILE_T // CONV_ROWS):
        r0 = rc * CONV_ROWS
        acc = jnp.zeros((CONV_ROWS, D_CF), F32) + bias
        for kk in range(CF_WIDTH):
            a, b = divmod(base + kk, V7X_SUBLANES)
            lo = r0 + a * V7X_SUBLANES
            if b == 0:
                tap = gbuf_ref[lo:lo + CONV_ROWS, :]
            else:
                tap = gsh_ref[b - 1, lo:lo + CONV_ROWS, :]
            acc = acc + cfw_ref[kk:kk + 1, :] * tap
        mu = jnp.mean(acc, axis=-1, keepdims=True)
        ac = acc - mu
        var = jnp.mean(ac * ac, axis=-1, keepdims=True)
        y = ac * lax.rsqrt(var + EPS) * ln_g + ln_b
        f_g = p_ref[0, r0:r0 + CONV_ROWS, OFF_FG:OFF_FG + D_CF].astype(F32)
        mixed_ref[r0:r0 + CONV_ROWS, D_RET + D_SC:D_RET + D_SC + D_CF] = (
            _silu(f_g) * _silu(y)).astype(BF16)
    gbuf_ref[0:CF_HALO, :] = gbuf_ref[TILE_T:TILE_T + CF_HALO, :]

    y = jnp.dot(mixed_ref[...], wout_ref[0], preferred_element_type=F32)
    res = x_ref[0] + gate_ref[0] * y
    if final_norm:
        ms = jnp.mean(res * res, axis=-1, keepdims=True)
        res = res * lax.rsqrt(ms + EPS) * fing_ref[...]
    o_ref[0] = res


def _mixer_layer(x, proj, cos, sin, gate, w_out_bf16, sc_w, cf_w, cf_b, ln_g, ln_b,
                 intra, qdec, kdec, cdec, final_g, layer, final_norm):
    bsz, t, d = x.shape
    const2 = lambda b, i: (0, 0)
    const3 = lambda b, i: (0, 0, 0)
    row = lambda b, i: (b, i, 0)
    return pl.pallas_call(
        functools.partial(_mixer_kernel, final_norm=final_norm),
        grid=(bsz, t // TILE_T),
        in_specs=[
            pl.BlockSpec(memory_space=pltpu.SMEM),
            pl.BlockSpec((1, TILE_T, d), row),
            pl.BlockSpec((1, TILE_T, D_IN), row),
            pl.BlockSpec((1, TILE_T, HALF_DIM), row),
            pl.BlockSpec((1, TILE_T, HALF_DIM), row),
            pl.BlockSpec((1, 1, d), lambda b, i: (b, 0, 0)),
            pl.BlockSpec((1, d, d), lambda b, i: (layer, 0, 0)),
            pl.BlockSpec((SC_WIDTH, D_SC), const2),
            pl.BlockSpec((CF_WIDTH, D_CF), const2),
            pl.BlockSpec((1, D_CF), const2),
            pl.BlockSpec((1, D_CF), const2),
            pl.BlockSpec((1, D_CF), const2),
            pl.BlockSpec((RET_HEADS, RET_CHUNK, RET_CHUNK), const3),
            pl.BlockSpec((RET_HEADS, RET_CHUNK, HALF_DIM), const3),
            pl.BlockSpec((RET_HEADS, RET_CHUNK, HALF_DIM), const3),
            pl.BlockSpec((1, d), const2),
        ],
        out_specs=pl.BlockSpec((1, TILE_T, d), row),
        out_shape=jax.ShapeDtypeStruct((bsz, t, d), F32),
        scratch_shapes=[
            pltpu.VMEM((RET_HEADS, RET_HEAD_DIM, RET_HEAD_DIM), F32),
            pltpu.VMEM((TILE_T, d), BF16),
            pltpu.VMEM((SC_HALO + TILE_T, D_SC), F32),
            pltpu.VMEM((CF_ROWS, D_CF), F32),
            pltpu.VMEM((V7X_SUBLANES - 1, CF_ROWS, D_CF), F32),
        ],
        compiler_params=pltpu.CompilerParams(
            dimension_semantics=("arbitrary", "arbitrary"),
            vmem_limit_bytes=V7X_VMEM_LIMIT_BYTES),
        name="mixer_layer",
    )(cdec, x, proj, cos, sin, gate, w_out_bf16, sc_w, cf_w, cf_b.reshape(1, D_CF),
      ln_g.reshape(1, D_CF), ln_b.reshape(1, D_CF), intra, qdec, kdec, final_g.reshape(1, d))


def _decay_tables():
    log_gamma = jnp.log1p(-jnp.exp2(-5.0 - jnp.arange(RET_HEADS, dtype=F32)))
    idx = jnp.arange(RET_CHUNK, dtype=F32)
    diff = idx[:, None] - idx[None, :]
    intra = jnp.where(diff[None] >= 0,
                      jnp.exp(jnp.maximum(diff, 0.0)[None] * log_gamma[:, None, None]), 0.0)
    qdec = jnp.exp((idx[None] + 1.0) * log_gamma[:, None])
    kdec = jnp.exp((RET_CHUNK - 1.0 - idx[None]) * log_gamma[:, None])
    cdec = jnp.exp(RET_CHUNK * log_gamma)
    bcast = lambda a: jnp.broadcast_to(a[:, :, None], (RET_HEADS, RET_CHUNK, HALF_DIM))
    return intra, bcast(qdec), bcast(kdec), cdec


def kernel(x, c, positions, ada_w, ada_b, norm_g, w_in, sc_conv_w, cf_conv_w, cf_conv_b, cf_ln_g, cf_ln_b, w_out, final_g):
    bsz, t, d = x.shape
    depth = ada_w.shape[0]
    assert d == D_MODEL and w_in.shape[2] == D_IN
    assert t % TILE_M == 0 and t % TILE_T == 0 and t % ROPE_TILE_T == 0 and D_IN % TILE_N == 0

    c_pad = jnp.zeros((V7X_SUBLANES, d), F32).at[:bsz].set(c)
    mod = _ada_mod(c_pad, ada_w, ada_b)[:, :bsz]
    shift = mod[:, :, :d].reshape(depth, bsz, 1, d)
    scale = mod[:, :, d:2 * d].reshape(depth, bsz, 1, d)
    gate = mod[:, :, 2 * d:].reshape(depth, bsz, 1, d)

    inv_freq = ROPE_BASE ** (-jnp.arange(0, HALF_DIM, dtype=F32) / HALF_DIM)
    cos, sin = _rope_tables(positions, inv_freq)
    intra, qdec, kdec, cdec = _decay_tables()

    w_in_bf16 = w_in.astype(BF16)
    w_out_bf16 = w_out.astype(BF16)

    h = x
    for layer in range(depth):
        proj = _in_projection(h, scale[layer], shift[layer], norm_g[layer], w_in_bf16, layer)
        h = _mixer_layer(h, proj, cos, sin, gate[layer], w_out_bf16, sc_conv_w[layer],
                         cf_conv_w[layer], cf_conv_b[layer], cf_ln_g[layer], cf_ln_b[layer],
                         intra, qdec, kdec, cdec, final_g, layer,
                         final_norm=(layer == depth - 1))
    return h
```
